```python
import math
import jax, jax.numpy as jnp
from jax import lax
import numpy as np

D_MODEL = 2048
BATCH = 2
SEQ = 16384
DEPTH = 2

GRID_W = 64
N_HEADS = 16
HEAD_DIM = 64
ATT_W = N_HEADS * HEAD_DIM
KH_MAX = 8
KW = 16
CONV_W = 1024
CONV_K = 31
PLE_DIM = 256
N_EXPERTS = 16
CAP_FACTOR = 2
D_EXPERT = 2048
N_IN = 3 * ATT_W + 2 * CONV_W + 2 * D_MODEL
EPS = 1e-6

kernel_name = "hybrid_natten_conformer_ecmoe_block"


def _rms(x):
    xf = x.astype(jnp.float32)
    return xf * lax.rsqrt(jnp.mean(xf * xf, axis=-1, keepdims=True) + EPS)


def rmsnorm(x, g):
    return (_rms(x) * g.astype(jnp.float32)).astype(x.dtype)


def layernorm(x, g, b):
    xf = x.astype(jnp.float32)
    mu = jnp.mean(xf, axis=-1, keepdims=True)
    var = jnp.mean(jnp.square(xf - mu), axis=-1, keepdims=True)
    y = (xf - mu) * lax.rsqrt(var + EPS) * g.astype(jnp.float32) + b.astype(jnp.float32)
    return y.astype(x.dtype)


def neighbourhood_attention(q, k, v, q_g, k_g, rpb):
    B, S, _ = q.shape
    rows = S // GRID_W
    kh = min(KH_MAX, rows)
    shp = (B, rows, GRID_W, N_HEADS, HEAD_DIM)
    q = rmsnorm(q.reshape(shp), q_g)
    k = rmsnorm(k.reshape(shp), k_g)
    v = v.reshape(shp)
    scale = 1.0 / math.sqrt(HEAD_DIM)

    cols = np.arange(GRID_W)
    col_start = np.clip(cols - KW // 2, 0, GRID_W - KW)
    col_idx = col_start[:, None] + np.arange(KW)[None, :]
    dc = col_idx - cols[:, None]
    col_ix = jnp.asarray(dc + KW - 1)[:, None, :]
    col_idx = jnp.asarray(col_idx)

    def one_row(args):
        q_r, r = args
        start = jnp.clip(r - kh // 2, 0, rows - kh)
        k_rows = lax.dynamic_slice_in_dim(k, start, kh, axis=1)
        v_rows = lax.dynamic_slice_in_dim(v, start, kh, axis=1)
        k_win = k_rows[:, :, col_idx]
        v_win = v_rows[:, :, col_idx]
        s = jnp.einsum('bqhd,biqjhd->bhqij', q_r, k_win).astype(jnp.float32) * scale
        dr = start + jnp.arange(kh) - r
        row_ix = (dr + KH_MAX - 1)[None, :, None]
        bias = rpb[:, row_ix, col_ix].astype(jnp.float32)
        s = s + bias[None]
        pr = jax.nn.softmax(s.reshape(B, N_HEADS, GRID_W, kh * KW), axis=-1)
        pr = pr.reshape(B, N_HEADS, GRID_W, kh, KW).astype(v.dtype)
        return jnp.einsum('bhqij,biqjhd->bqhd', pr, v_win)

    q_rows = jnp.transpose(q, (1, 0, 2, 3, 4))
    out = lax.map(one_row, (q_rows, jnp.arange(rows)))
    return jnp.transpose(out, (1, 0, 2, 3, 4)).reshape(B, S, ATT_W)


def conformer_conv(a, b, w_dw, b_dw, ln_g, ln_b):
    u = a * jax.nn.sigmoid(b)
    u = lax.conv_general_dilated(
        u, w_dw[:, None, :].astype(u.dtype), window_strides=(1,),
        padding=[(CONV_K // 2, CONV_K // 2)],
        dimension_numbers=('NWC', 'WIO', 'NWC'),
        feature_group_count=CONV_W) + b_dw
    u = layernorm(u, ln_g, ln_b)
    return jax.nn.silu(u)


def expert_choice_moe(h, w_router, w1, w3, w2):
    B, S, _ = h.shape
    cap = CAP_FACTOR * S // N_EXPERTS
    logits = jnp.einsum('bsd,de->bse', h, w_router).astype(jnp.float32)
    aff = jax.nn.softmax(logits, axis=-1)
    gate, idx = lax.top_k(jnp.transpose(aff, (0, 2, 1)), cap)
    bidx = jnp.arange(B)[:, None, None]
    xe = h[bidx, idx]
    hid = jax.nn.silu(jnp.einsum('becd,edf->becf', xe, w1)) * jnp.einsum('becd,edf->becf', xe, w3)
    ye = jnp.einsum('becf,efd->becd', hid, w2) * gate[..., None].astype(h.dtype)
    return jnp.zeros_like(h).at[bidx, idx].add(ye)


def setup_inputs(seed: int = 0) -> dict:
    key = jax.random.key(seed)
    ks = jax.random.split(key, 24)
    L, D, C, E, F = DEPTH, D_MODEL, CONV_W, N_EXPERTS, D_EXPERT
    nrm = lambda k, shape, s: jax.random.normal(k, shape, jnp.float32) * s
    gain = lambda k, shape: 1.0 + 0.05 * jax.random.normal(k, shape, jnp.float32)
    return {
        "x": nrm(ks[0], (BATCH, SEQ, D), 1.0),
        "p": nrm(ks[1], (DEPTH, BATCH, SEQ, PLE_DIM), 1.0),
        "norm1_g": gain(ks[2], (L, D)),
        "w_in": nrm(ks[3], (L, D, N_IN), D ** -0.5),
        "q_norm_g": gain(ks[4], (L, HEAD_DIM)),
        "k_norm_g": gain(ks[5], (L, HEAD_DIM)),
        "rpb": nrm(ks[6], (L, N_HEADS, 2 * KH_MAX - 1, 2 * KW - 1), 0.1),
        "w_dw": nrm(ks[7], (L, CONV_K, C), CONV_K ** -0.5),
        "b_dw": nrm(ks[8], (L, C), 0.02),
        "conv_ln_g": gain(ks[9], (L, C)),
        "conv_ln_b": nrm(ks[10], (L, C), 0.02),
        "w_att_o": nrm(ks[11], (L, ATT_W, D), ATT_W ** -0.5),
        "w_conv_o": nrm(ks[12], (L, C, D), C ** -0.5),
        "w_out": nrm(ks[13], (L, D, D), D ** -0.5),
        "norm2_g": gain(ks[14], (L, D)),
        "w_router": nrm(ks[15], (L, D, E), D ** -0.5),
        "w1": nrm(ks[16], (L, E, D, F), D ** -0.5),
        "w3": nrm(ks[17], (L, E, D, F), D ** -0.5),
        "w2": nrm(ks[18], (L, E, F, D), F ** -0.5),
        "w_pe": nrm(ks[19], (L, PLE_DIM, D), PLE_DIM ** -0.5),
        "pe_norm_g": gain(ks[20], (L, D)),
        "w_pg": nrm(ks[21], (L, D, D), D ** -0.5),
    }


def reference(x, p, norm1_g, w_in, q_norm_g, k_norm_g, rpb, w_dw, b_dw, conv_ln_g,
              conv_ln_b, w_att_o, w_conv_o, w_out, norm2_g, w_router, w1, w3, w2,
              w_pe, pe_norm_g, w_pg):
    o_k, o_v = ATT_W, 2 * ATT_W
    o_a, o_b = 3 * ATT_W, 3 * ATT_W + CONV_W
    o_ga = 3 * ATT_W + 2 * CONV_W
    o_gc = o_ga + D_MODEL
    for i in range(DEPTH):
        h = rmsnorm(x, norm1_g[i])
        u = jnp.einsum('bsd,dn->bsn', h, w_in[i])
        q, k, v = u[..., :o_k], u[..., o_k:o_v], u[..., o_v:o_a]
        ca, cb = u[..., o_a:o_b], u[..., o_b:o_ga]
        g_att = jax.nn.sigmoid(u[..., o_ga:o_gc])
        g_conv = jax.nn.sigmoid(u[..., o_gc:])
        y_att = neighbourhood_attention(q, k, v, q_norm_g[i], k_norm_g[i], rpb[i]) @ w_att_o[i]
        y_conv = conformer_conv(ca, cb, w_dw[i], b_dw[i], conv_ln_g[i], conv_ln_b[i]) @ w_conv_o[i]
        x = x + (g_att * y_att + g_conv * y_conv) @ w_out[i]
        x = x + expert_choice_moe(rmsnorm(x, norm2_g[i]), w_router[i], w1[i], w3[i], w2[i])
        gate = jax.nn.sigmoid(jnp.einsum('bsd,de->bse', _rms(x).astype(x.dtype), w_pg[i]))
        pe = rmsnorm(jnp.einsum('bsk,kd->bsd', p[i], w_pe[i]), pe_norm_g[i])
        x = x + gate * pe
    return x
```

```python
import functools
import math

import jax
import jax.numpy as jnp
import numpy as np
from jax import lax
from jax.experimental import pallas as pl
from jax.experimental.pallas import tpu as pltpu

GRID_W = 64
N_HEADS = 16
HEAD_DIM = 64
ATT_W = N_HEADS * HEAD_DIM
KH = 8
KW = 16
CONV_W = 1024
CONV_K = 31
CAP_FACTOR = 2
EPS = 1e-6

V7X_VMEM_BYTES = 64 * 1024 * 1024
VMEM_LIMIT = V7X_VMEM_BYTES * 3 // 4
BF16_SUBLANES = 16
ROWS_PER_ATT_BLOCK = 8
ATT_WIN_ROWS = ROWS_PER_ATT_BLOCK + KH - 1
NEG_INF = -1e30

f32 = jnp.float32
bf16 = jnp.bfloat16


def _params(*sem):
    return pltpu.CompilerParams(dimension_semantics=sem, vmem_limit_bytes=VMEM_LIMIT)


def _sigmoid(x):
    return 1.0 / (1.0 + jnp.exp(-x))


def _rms_kernel(x_ref, g_ref, o_ref):
    x = x_ref[...]
    ms = jnp.mean(x * x, axis=-1, keepdims=True)
    o_ref[...] = (x * lax.rsqrt(ms + EPS) * g_ref[...]).astype(o_ref.dtype)


def _rmsnorm(x, g, tm):
    t, d = x.shape
    return pl.pallas_call(
        _rms_kernel,
        grid=(t // tm,),
        in_specs=[pl.BlockSpec((tm, d), lambda i: (i, 0)),
                  pl.BlockSpec((1, d), lambda i: (0, 0))],
        out_specs=pl.BlockSpec((tm, d), lambda i: (i, 0)),
        out_shape=jax.ShapeDtypeStruct((t, d), bf16),
        compiler_params=_params("parallel"),
        name="rmsnorm",
    )(x, g.reshape(1, d))


def _mm_kernel(*refs, n_pairs, n_extras, epilogue):
    a_refs = refs[:n_pairs]
    w_refs = refs[n_pairs:2 * n_pairs]
    e_refs = refs[2 * n_pairs:2 * n_pairs + n_extras]
    o_ref = refs[2 * n_pairs + n_extras]
    accs = [jnp.dot(a[...], w[...], preferred_element_type=f32) for a, w in zip(a_refs, w_refs)]
    o_ref[...] = epilogue(accs, [e[...] for e in e_refs]).astype(o_ref.dtype)


def _matmul(pairs, extras, epilogue, n, out_dtype, tm, tn, name):
    t = pairs[0][0].shape[0]
    in_specs, args = [], []
    for a, _, _ in pairs:
        in_specs.append(pl.BlockSpec((tm, a.shape[1]), lambda i, j: (i, 0)))
        args.append(a)
    for _, w, off in pairs:
        in_specs.append(pl.BlockSpec((w.shape[0], tn), lambda i, j, off=off: (0, j + off)))
        args.append(w)
    for arr, kind, off in extras:
        if kind == "row":
            in_specs.append(pl.BlockSpec((1, tn), lambda i, j, off=off: (0, j + off)))
        elif kind == "tile":
            in_specs.append(pl.BlockSpec((tm, tn), lambda i, j, off=off: (i, j + off)))
        else:
            in_specs.append(pl.BlockSpec(arr.shape, lambda i, j: (0, 0)))
        args.append(arr)
    kern = functools.partial(_mm_kernel, n_pairs=len(pairs), n_extras=len(extras), epilogue=epilogue)
    return pl.pallas_call(
        kern,
        grid=(t // tm, n // tn),
        in_specs=in_specs,
        out_specs=pl.BlockSpec((tm, tn), lambda i, j: (i, j)),
        out_shape=jax.ShapeDtypeStruct((t, n), out_dtype),
        compiler_params=_params("parallel", "arbitrary"),
        name=name,
    )(*args)


def _split_bf16(x):
    hi = x.astype(bf16)
    lo = (x - hi.astype(f32)).astype(bf16)
    return hi, lo


def _epi_qknorm(accs, extras):
    acc = accs[0]
    gain, head_ones = extras
    hi, lo = _split_bf16(acc * acc)
    ss = (jnp.dot(hi, head_ones, preferred_element_type=f32)
          + jnp.dot(lo, head_ones, preferred_element_type=f32))
    return acc * lax.rsqrt(ss * (1.0 / HEAD_DIM) + EPS) * gain


def _epi_plain(accs, extras):
    return accs[0]


def _epi_glu(accs, extras):
    return accs[0] * _sigmoid(accs[1])


def _epi_sigmoid(accs, extras):
    return _sigmoid(accs[0])


def _epi_mix(accs, extras):
    return extras[0].astype(f32) * accs[0] + extras[1].astype(f32) * accs[1]


def _epi_residual(accs, extras):
    return extras[0] + accs[0]


def _att_window_start(rb, rows):
    return jnp.clip(rb * ROWS_PER_ATT_BLOCK - KH // 2, 0, rows - ATT_WIN_ROWS)


def _att_row_start(r, rows):
    return jnp.clip(r - KH // 2, 0, rows - KH)


def _attn_kernel(q_ref, k_ref, v_ref, bias_ref, o_ref, *, rows):
    rb = pl.program_id(1)
    r = rb * ROWS_PER_ATT_BLOCK + pl.program_id(2)
    loc = _att_row_start(r, rows) - _att_window_start(rb, rows)
    lane = lax.broadcasted_iota(jnp.int32, (GRID_W, 2 * HEAD_DIM), 1)
    first_head = lane < HEAD_DIM
    n_keys = KH * GRID_W
    for p in range(N_HEADS // 2):
        cols = slice(p * 2 * HEAD_DIM, (p + 1) * 2 * HEAD_DIM)
        qp = q_ref[:, cols]
        kp = k_ref[0, pl.ds(loc, KH), :, cols].reshape(n_keys, 2 * HEAD_DIM)
        vp = v_ref[0, pl.ds(loc, KH), :, cols].reshape(n_keys, 2 * HEAD_DIM)
        zero = jnp.zeros_like(qp)
        q2 = jnp.concatenate([jnp.where(first_head, qp, zero), jnp.where(first_head, zero, qp)], axis=0)
        s = lax.dot_general(q2, kp, (((1,), (1,)), ((), ())), preferred_element_type=f32)
        s = s + bias_ref[p]
        m = jnp.max(s, axis=-1, keepdims=True)
        e = jnp.exp(s - m)
        denom = jnp.sum(e, axis=-1, keepdims=True)
        pv = jnp.dot(e.astype(bf16), vp, preferred_element_type=f32) * (1.0 / denom)
        o_ref[:, cols] = jnp.where(first_head, pv[:GRID_W], pv[GRID_W:]).astype(o_ref.dtype)


def _attention_bias(rpb):
    cols = np.arange(GRID_W)
    col_start = np.clip(cols - KW // 2, 0, GRID_W - KW)
    kc = np.arange(GRID_W)
    in_win = (kc[None, :] >= col_start[:, None]) & (kc[None, :] < col_start[:, None] + KW)
    dc = np.clip(kc[None, :] - cols[:, None] + KW - 1, 0, 2 * KW - 2)
    o = np.arange(KH)
    i = np.arange(KH)
    row_ix = i[None, :] - o[:, None] + KH - 1
    tab = rpb[:, row_ix[:, None, :, None], dc[None, :, None, :]]
    tab = jnp.where(jnp.asarray(in_win)[None, None, :, None, :], tab.astype(f32), NEG_INF)
    tab = jnp.transpose(tab, (1, 0, 2, 3, 4)).reshape(KH, N_HEADS // 2, 2 * GRID_W, KH * GRID_W)
    return tab


def _attention(qk, v, bias, b, s):
    rows = s // GRID_W
    assert rows % ROWS_PER_ATT_BLOCK == 0 and rows >= ATT_WIN_ROWS
    qk4 = qk.reshape(b, rows, GRID_W, 2 * ATT_W)
    v4 = v.reshape(b, rows, GRID_W, ATT_W)
    rpb_blocks = rows // ROWS_PER_ATT_BLOCK

    def q_map(bi, rb, rr):
        return (bi, rb * ROWS_PER_ATT_BLOCK + rr, 0, 0)

    def k_map(bi, rb, rr):
        return (bi, _att_window_start(rb, rows), 0, ATT_W)

    def v_map(bi, rb, rr):
        return (bi, _att_window_start(rb, rows), 0, 0)

    win_block = tuple(pl.Element(n) for n in (1, ATT_WIN_ROWS, GRID_W, ATT_W))

    def bias_map(bi, rb, rr):
        r = rb * ROWS_PER_ATT_BLOCK + rr
        return (r - _att_row_start(r, rows), 0, 0, 0)

    def o_map(bi, rb, rr):
        return (bi, rb * ROWS_PER_ATT_BLOCK + rr, 0, 0)

    out = pl.pallas_call(
        functools.partial(_attn_kernel, rows=rows),
        grid=(b, rpb_blocks, ROWS_PER_ATT_BLOCK),
        in_specs=[
            pl.BlockSpec((None, None, GRID_W, ATT_W), q_map),
            pl.BlockSpec(win_block, k_map),
            pl.BlockSpec(win_block, v_map),
            pl.BlockSpec((None, N_HEADS // 2, 2 * GRID_W, KH * GRID_W), bias_map),
        ],
        out_specs=pl.BlockSpec((None, None, GRID_W, ATT_W), o_map),
        out_shape=jax.ShapeDtypeStruct((b, rows, GRID_W, ATT_W), bf16),
        compiler_params=_params("parallel", "parallel", "arbitrary"),
        name="natten",
    )(qk4, qk4, v4, bias)
    return out.reshape(b * s, ATT_W)


CONV_HALO = 16
CONV_CHUNK = 16


def _conv_kernel(prev_ref, main_ref, next_ref, w_ref, b_ref, g_ref, beta_ref, o_ref, buf_ref, *, ts, nblk):
    i = pl.program_id(1)
    buf_ref[0:CONV_HALO, :] = jnp.where(i > 0, prev_ref[...].astype(f32), 0.0)
    buf_ref[CONV_HALO:CONV_HALO + ts, :] = main_ref[...].astype(f32)
    buf_ref[CONV_HALO + ts:2 * CONV_HALO + ts, :] = jnp.where(i < nblk - 1, next_ref[...].astype(f32), 0.0)
    first = CONV_HALO - CONV_K // 2
    for c in range(ts // CONV_CHUNK):
        t0 = c * CONV_CHUNK
        acc = buf_ref[t0 + first:t0 + first + CONV_CHUNK, :] * w_ref[0:1, :]
        for k in range(1, CONV_K):
            acc = acc + buf_ref[t0 + first + k:t0 + first + k + CONV_CHUNK, :] * w_ref[k:k + 1, :]
        y = acc + b_ref[...]
        mu = jnp.mean(y, axis=-1, keepdims=True)
        yc = y - mu
        var = jnp.mean(yc * yc, axis=-1, keepdims=True)
        yn = yc * lax.rsqrt(var + EPS) * g_ref[...] + beta_ref[...]
        o_ref[t0:t0 + CONV_CHUNK, :] = (yn * _sigmoid(yn)).astype(o_ref.dtype)


def _conformer_conv(u, w_dw, b_dw, ln_g, ln_b, b, s, ts):
    c = u.shape[-1]
    u3 = u.reshape(b, s, c)
    nblk = s // ts
    halo_per_blk = ts // CONV_HALO
    n_halo = s // CONV_HALO
    row = lambda a: a.reshape(1, c).astype(f32)
    out = pl.pallas_call(
        functools.partial(_conv_kernel, ts=ts, nblk=nblk),
        grid=(b, nblk),
        in_specs=[
            pl.BlockSpec((None, CONV_HALO, c), lambda bi, i: (bi, jnp.maximum(i * halo_per_blk - 1, 0), 0)),
            pl.BlockSpec((None, ts, c), lambda bi, i: (bi, i, 0)),
            pl.BlockSpec((None, CONV_HALO, c), lambda bi, i: (bi, jnp.minimum((i + 1) * halo_per_blk, n_halo - 1), 0)),
            pl.BlockSpec((CONV_K, c), lambda bi, i: (0, 0)),
            pl.BlockSpec((1, c), lambda bi, i: (0, 0)),
            pl.BlockSpec((1, c), lambda bi, i: (0, 0)),
            pl.BlockSpec((1, c), lambda bi, i: (0, 0)),
        ],
        out_specs=pl.BlockSpec((None, ts, c), lambda bi, i: (bi, i, 0)),
        out_shape=jax.ShapeDtypeStruct((b, s, c), bf16),
        scratch_shapes=[pltpu.VMEM((ts + 2 * CONV_HALO, c), f32)],
        compiler_params=_params("parallel", "parallel"),
        name="conformer_conv",
    )(u3, u3, u3, w_dw.astype(f32), row(b_dw), row(ln_g), row(ln_b))
    return out.reshape(b * s, c)


def _router_kernel(x_ref, g_ref, wr_ref, h_ref, aff_ref):
    x = x_ref[...]
    ms = jnp.mean(x * x, axis=-1, keepdims=True)
    h = x * lax.rsqrt(ms + EPS) * g_ref[...]
    h_ref[...] = h.astype(h_ref.dtype)
    logits = lax.dot_general(wr_ref[...], h, (((1,), (1,)), ((), ())),
                             preferred_element_type=f32, precision=lax.Precision.HIGHEST)
    m = jnp.max(logits, axis=0, keepdims=True)
    e = jnp.exp(logits - m)
    aff_ref[...] = e / jnp.sum(e, axis=0, keepdims=True)


def _router(x, g, w_router, b, s, tm):
    t, d = x.shape
    e = w_router.shape[-1]
    blocks_per_seq = s // tm
    h, aff = pl.pallas_call(
        _router_kernel,
        grid=(t // tm,),
        in_specs=[pl.BlockSpec((tm, d), lambda i: (i, 0)),
                  pl.BlockSpec((1, d), lambda i: (0, 0)),
                  pl.BlockSpec((e, d), lambda i: (0, 0))],
        out_specs=[pl.BlockSpec((tm, d), lambda i: (i, 0)),
                   pl.BlockSpec((None, e, tm), lambda i: (i // blocks_per_seq, 0, i % blocks_per_seq))],
        out_shape=[jax.ShapeDtypeStruct((t, d), f32),
                   jax.ShapeDtypeStruct((b, e, s), f32)],
        compiler_params=_params("parallel"),
        name="router",
    )(x, g.reshape(1, d), jnp.transpose(w_router))
    return h, aff


MOE_ROWS = 1024
MOE_RMW_ROWS = 256


def _moe_kernel(idx_ref, h_hbm, gate_ref, w1_ref, w3_ref, w2_ref, x_in_hbm, x_hbm,
                xg_ref, xb_ref, acc_ref, sem_g, sem_r, sem_w, *, seq, tc, rmw, n_f):
    del x_in_hbm
    fi = pl.program_id(3)
    base = pl.program_id(1) * seq

    def token(c):
        return base + idx_ref[0, 0, c]

    def loop(n, fn):
        def body(c, carry):
            fn(c)
            return carry
        lax.fori_loop(0, n, body, 0)

    def gather_copy(c):
        return pltpu.make_async_copy(h_hbm.at[pl.ds(token(c), 1)], xg_ref.at[pl.ds(c, 1)], sem_g)

    @pl.when(fi == 0)
    def _gather():
        loop(tc, lambda c: gather_copy(c).start())
        loop(tc, lambda c: gather_copy(c).wait())
        xb_ref[...] = xg_ref[...].astype(bf16)

    xb = xb_ref[...]
    h1 = jnp.dot(xb, w1_ref[...], preferred_element_type=f32)
    h3 = jnp.dot(xb, w3_ref[...], preferred_element_type=f32)
    hid = (h1 * _sigmoid(h1) * h3).astype(bf16)
    contrib = jnp.dot(hid, w2_ref[...], preferred_element_type=f32)

    @pl.when(fi == 0)
    def _init():
        acc_ref[...] = contrib

    @pl.when(fi > 0)
    def _accumulate():
        acc_ref[...] += contrib

    @pl.when(fi == n_f - 1)
    def _scatter_add():
        def read_copy(c0, c):
            return pltpu.make_async_copy(x_hbm.at[pl.ds(token(c0 + c), 1)], xg_ref.at[pl.ds(c0 + c, 1)], sem_r)

        def write_copy(c0, c):
            return pltpu.make_async_copy(xg_ref.at[pl.ds(c0 + c, 1)], x_hbm.at[pl.ds(token(c0 + c), 1)], sem_w)

        for chunk in range(tc // rmw):
            c0 = chunk * rmw
            rows = pl.ds(c0, rmw)
            loop(rmw, lambda c: read_copy(c0, c).start())
            loop(rmw, lambda c: read_copy(c0, c).wait())
            xg_ref[rows, :] = xg_ref[rows, :] + acc_ref[rows, :] * gate_ref[rows, :]
            loop(rmw, lambda c: write_copy(c0, c).start())
            loop(rmw, lambda c: write_copy(c0, c).wait())


def _moe(x, h, idx, gate, w1, w3, w2, b, s, tf):
    t, d = x.shape
    e, _, f = w1.shape
    cap = idx.shape[-1]
    n_f = f // tf
    tc = _tile(cap, MOE_ROWS)
    rmw = _tile(tc, MOE_RMW_ROWS)
    n_c = cap // tc
    kern = functools.partial(_moe_kernel, seq=s, tc=tc, rmw=rmw, n_f=n_f)
    return pl.pallas_call(
        kern,
        grid=(e, b, n_c, n_f),
        in_specs=[
            pl.BlockSpec((1, 1, tc), lambda ei, bi, ci, fi: ((bi * e + ei) * n_c + ci, 0, 0),
                         memory_space=pltpu.SMEM),
            pl.BlockSpec(memory_space=pl.ANY),
            pl.BlockSpec((None, None, tc, 1), lambda ei, bi, ci, fi: (bi, ei, ci, 0)),
            pl.BlockSpec((None, d, tf), lambda ei, bi, ci, fi: (ei, 0, fi)),
            pl.BlockSpec((None, d, tf), lambda ei, bi, ci, fi: (ei, 0, fi)),
            pl.BlockSpec((None, tf, d), lambda ei, bi, ci, fi: (ei, fi, 0)),
            pl.BlockSpec(memory_space=pl.ANY),
        ],
        out_specs=pl.BlockSpec(memory_space=pl.ANY),
        out_shape=jax.ShapeDtypeStruct((t, d), f32),
        scratch_shapes=[
            pltpu.VMEM((tc, d), f32),
            pltpu.VMEM((tc, d), bf16),
            pltpu.VMEM((tc, d), f32),
            pltpu.SemaphoreType.DMA(()),
            pltpu.SemaphoreType.DMA(()),
            pltpu.SemaphoreType.DMA(()),
        ],
        input_output_aliases={6: 0},
        compiler_params=_params("arbitrary", "arbitrary", "arbitrary", "arbitrary"),
        name="moe_ffn",
    )(idx.reshape(b * e * n_c, 1, tc), h, gate.reshape(b, e, cap, 1), w1, w3, w2, x)


def _ple_kernel(x_ref, p_ref, wpg_ref, wpe_ref, g_ref, o_ref):
    x = x_ref[...]
    ms = jnp.mean(x * x, axis=-1, keepdims=True)
    xn = (x * lax.rsqrt(ms + EPS)).astype(bf16)
    gate = _sigmoid(jnp.dot(xn, wpg_ref[...], preferred_element_type=f32))
    pe = jnp.dot(p_ref[...].astype(bf16), wpe_ref[...], preferred_element_type=f32)
    pms = jnp.mean(pe * pe, axis=-1, keepdims=True)
    o_ref[...] = x + gate * (pe * lax.rsqrt(pms + EPS) * g_ref[...])


def _ple(x, p, w_pg, w_pe, g, tm):
    t, d = x.shape
    kp = p.shape[-1]
    return pl.pallas_call(
        _ple_kernel,
        grid=(t // tm,),
        in_specs=[pl.BlockSpec((tm, d), lambda i: (i, 0)),
                  pl.BlockSpec((tm, kp), lambda i: (i, 0)),
                  pl.BlockSpec((d, d), lambda i: (0, 0)),
                  pl.BlockSpec((kp, d), lambda i: (0, 0)),
                  pl.BlockSpec((1, d), lambda i: (0, 0))],
        out_specs=pl.BlockSpec((tm, d), lambda i: (i, 0)),
        out_shape=jax.ShapeDtypeStruct((t, d), f32),
        compiler_params=_params("parallel"),
        name="ple",
    )(x, p, w_pg, w_pe, g.reshape(1, d))


def _tile(n, pref):
    return pref if n % pref == 0 else n


def kernel(x, p, norm1_g, w_in, q_norm_g, k_norm_g, rpb, w_dw, b_dw, conv_ln_g, conv_ln_b, w_att_o,
           w_conv_o, w_out, norm2_g, w_router, w1, w3, w2, w_pe, pe_norm_g, w_pg):
    b, s, d = x.shape
    depth = w_in.shape[0]
    t = b * s
    e = w_router.shape[-1]
    f = w1.shape[-1]
    cap = CAP_FACTOR * s // e
    o_v, o_a, o_b, o_g = 2 * ATT_W, 3 * ATT_W, 3 * ATT_W + CONV_W, 3 * ATT_W + 2 * CONV_W

    tm = _tile(t, 1024)
    tn = 512
    tn_d = _tile(d, tn)
    head_ones = jnp.asarray(np.kron(np.eye(tn // HEAD_DIM), np.ones((HEAD_DIM, HEAD_DIM))), bf16)
    scale = 1.0 / math.sqrt(HEAD_DIM)

    x = x.reshape(t, d)
    for i in range(depth):
        w_in_i = w_in[i].astype(bf16)
        h = _rmsnorm(x, norm1_g[i], _tile(t, 512))

        qk_gain = jnp.concatenate([jnp.tile(q_norm_g[i], N_HEADS) * scale,
                                   jnp.tile(k_norm_g[i], N_HEADS)]).reshape(1, 2 * ATT_W).astype(f32)
        qk = _matmul([(h, w_in_i, 0)], [(qk_gain, "row", 0), (head_ones, "whole", 0)],
                     _epi_qknorm, 2 * ATT_W, bf16, tm, tn, "proj_qk")
        v = _matmul([(h, w_in_i, o_v // tn)], [], _epi_plain, ATT_W, bf16, tm, tn, "proj_v")
        u = _matmul([(h, w_in_i, o_a // tn), (h, w_in_i, o_b // tn)], [], _epi_glu,
                    CONV_W, bf16, tm, tn, "proj_glu")
        gates = _matmul([(h, w_in_i, o_g // tn_d)], [], _epi_sigmoid, 2 * d, bf16, tm, tn_d, "proj_gates")

        att = _attention(qk, v, _attention_bias(rpb[i]), b, s)
        cv = _conformer_conv(u, w_dw[i], b_dw[i], conv_ln_g[i], conv_ln_b[i], b, s, _tile(s, 256))

        mix = _matmul([(att, w_att_o[i].astype(bf16), 0), (cv, w_conv_o[i].astype(bf16), 0)],
                      [(gates, "tile", 0), (gates, "tile", d // tn_d)],
                      _epi_mix, d, bf16, tm, tn_d, "mix")
        x = _matmul([(mix, w_out[i].astype(bf16), 0)], [(x, "tile", 0)], _epi_residual,
                    d, f32, tm, tn_d, "out_proj")

        h2, aff = _router(x, norm2_g[i], w_router[i], b, s, _tile(s, 512))
        gate, idx = lax.top_k(aff, cap)
        x = _moe(x, h2, idx, gate, w1[i].astype(bf16), w3[i].astype(bf16), w2[i].astype(bf16),
                 b, s, _tile(f, 512))

        x = _ple(x, p[i].reshape(t, -1), w_pg[i].astype(bf16), w_pe[i].astype(bf16), pe_norm_g[i],
                 _tile(t, 256))
    return x.reshape(b, s, d)
```

```python
import functools
import math

import jax
import jax.numpy as jnp
import numpy as np
from jax import lax
from jax.experimental import pallas as pl
from jax.experimental.pallas import tpu as pltpu

GRID_W = 64
N_HEADS = 16
HEAD_DIM = 64
ATT_W = N_HEADS * HEAD_DIM
KH = 8
KW = 16
CONV_W = 1024
CONV_K = 31
CAP_FACTOR = 2
EPS = 1e-6

V7X_VMEM_BYTES = 64 * 1024 * 1024
VMEM_LIMIT = V7X_VMEM_BYTES * 3 // 4
BF16_SUBLANES = 16
ROWS_PER_ATT_BLOCK = 8
ATT_WIN_ROWS = ROWS_PER_ATT_BLOCK + KH - 1
NEG_INF = -1e30

f32 = jnp.float32
bf16 = jnp.bfloat16


def _params(*sem):
    return pltpu.CompilerParams(dimension_semantics=sem, vmem_limit_bytes=VMEM_LIMIT)


def _sigmoid(x):
    return 1.0 / (1.0 + jnp.exp(-x))


def _rms_kernel(x_ref, g_ref, o_ref):
    x = x_ref[...]
    ms = jnp.mean(x * x, axis=-1, keepdims=True)
    o_ref[...] = (x * lax.rsqrt(ms + EPS) * g_ref[...]).astype(o_ref.dtype)


def _rmsnorm(x, g, tm):
    t, d = x.shape
    return pl.pallas_call(
        _rms_kernel,
        grid=(t // tm,),
        in_specs=[pl.BlockSpec((tm, d), lambda i: (i, 0)),
                  pl.BlockSpec((1, d), lambda i: (0, 0))],
        out_specs=pl.BlockSpec((tm, d), lambda i: (i, 0)),
        out_shape=jax.ShapeDtypeStruct((t, d), bf16),
        compiler_params=_params("parallel"),
        name="rmsnorm",
    )(x, g.reshape(1, d))


def _mm_kernel(*refs, n_pairs, n_extras, epilogue):
    a_refs = refs[:n_pairs]
    w_refs = refs[n_pairs:2 * n_pairs]
    e_refs = refs[2 * n_pairs:2 * n_pairs + n_extras]
    o_ref = refs[2 * n_pairs + n_extras]
    accs = [jnp.dot(a[...], w[...], preferred_element_type=f32) for a, w in zip(a_refs, w_refs)]
    o_ref[...] = epilogue(accs, [e[...] for e in e_refs]).astype(o_ref.dtype)


def _matmul(pairs, extras, epilogue, n, out_dtype, tm, tn, name):
    t = pairs[0][0].shape[0]
    in_specs, args = [], []
    for a, _, _ in pairs:
        in_specs.append(pl.BlockSpec((tm, a.shape[1]), lambda i, j: (i, 0)))
        args.append(a)
    for _, w, off in pairs:
        in_specs.append(pl.BlockSpec((w.shape[0], tn), lambda i, j, off=off: (0, j + off)))
        args.append(w)
    for arr, kind, off in extras:
        if kind == "row":
            in_specs.append(pl.BlockSpec((1, tn), lambda i, j, off=off: (0, j + off)))
        elif kind == "tile":
            in_specs.append(pl.BlockSpec((tm, tn), lambda i, j, off=off: (i, j + off)))
        else:
            in_specs.append(pl.BlockSpec(arr.shape, lambda i, j: (0, 0)))
        args.append(arr)
    kern = functools.partial(_mm_kernel, n_pairs=len(pairs), n_extras=len(extras), epilogue=epilogue)
    return pl.pallas_call(
        kern,
        grid=(t // tm, n // tn),
        in_specs=in_specs,
        out_specs=pl.BlockSpec((tm, tn), lambda i, j: (i, j)),
        out_shape=jax.ShapeDtypeStruct((t, n), out_dtype),
        compiler_params=_params("parallel", "arbitrary"),
        name=name,
    )(*args)


def _split_bf16(x):
    hi = x.astype(bf16)
    lo = (x - hi.astype(f32)).astype(bf16)
    return hi, lo


def _epi_qknorm(accs, extras):
    acc = accs[0]
    gain, head_ones = extras
    hi, lo = _split_bf16(acc * acc)
    ss = (jnp.dot(hi, head_ones, preferred_element_type=f32)
          + jnp.dot(lo, head_ones, preferred_element_type=f32))
    return acc * lax.rsqrt(ss * (1.0 / HEAD_DIM) + EPS) * gain


def _epi_plain(accs, extras):
    return accs[0]


def _epi_glu(accs, extras):
    return accs[0] * _sigmoid(accs[1])


def _epi_sigmoid(accs, extras):
    return _sigmoid(accs[0])


def _epi_mix(accs, extras):
    return extras[0].astype(f32) * accs[0] + extras[1].astype(f32) * accs[1]


def _epi_residual(accs, extras):
    return extras[0] + accs[0]


def _att_window_start(rb, rows):
    return jnp.clip(rb * ROWS_PER_ATT_BLOCK - KH // 2, 0, rows - ATT_WIN_ROWS)


def _att_row_start(r, rows):
    return jnp.clip(r - KH // 2, 0, rows - KH)


def _attn_kernel(q_ref, k_ref, v_ref, bias_ref, o_ref, *, rows):
    rb = pl.program_id(1)
    r = rb * ROWS_PER_ATT_BLOCK + pl.program_id(2)
    loc = _att_row_start(r, rows) - _att_window_start(rb, rows)
    lane = lax.broadcasted_iota(jnp.int32, (GRID_W, 2 * HEAD_DIM), 1)
    first_head = lane < HEAD_DIM
    n_keys = KH * GRID_W
    for p in range(N_HEADS // 2):
        cols = slice(p * 2 * HEAD_DIM, (p + 1) * 2 * HEAD_DIM)
        qp = q_ref[:, cols]
        kp = k_ref[0, pl.ds(loc, KH), :, cols].reshape(n_keys, 2 * HEAD_DIM)
        vp = v_ref[0, pl.ds(loc, KH), :, cols].reshape(n_keys, 2 * HEAD_DIM)
        zero = jnp.zeros_like(qp)
        q2 = jnp.concatenate([jnp.where(first_head, qp, zero), jnp.where(first_head, zero, qp)], axis=0)
        s = lax.dot_general(q2, kp, (((1,), (1,)), ((), ())), preferred_element_type=f32)
        s = s + bias_ref[p]
        m = jnp.max(s, axis=-1, keepdims=True)
        e = jnp.exp(s - m)
        denom = jnp.sum(e, axis=-1, keepdims=True)
        pv = jnp.dot(e.astype(bf16), vp, preferred_element_type=f32) * (1.0 / denom)
        o_ref[:, cols] = jnp.where(first_head, pv[:GRID_W], pv[GRID_W:]).astype(o_ref.dtype)


def _attention_bias(rpb):
    cols = np.arange(GRID_W)
    col_start = np.clip(cols - KW // 2, 0, GRID_W - KW)
    kc = np.arange(GRID_W)
    in_win = (kc[None, :] >= col_start[:, None]) & (kc[None, :] < col_start[:, None] + KW)
    dc = np.clip(kc[None, :] - cols[:, None] + KW - 1, 0, 2 * KW - 2)
    o = np.arange(KH)
    i = np.arange(KH)
    row_ix = i[None, :] - o[:, None] + KH - 1
    tab = rpb[:, row_ix[:, None, :, None], dc[None, :, None, :]]
    tab = jnp.where(jnp.asarray(in_win)[None, None, :, None, :], tab.astype(f32), NEG_INF)
    tab = jnp.transpose(tab, (1, 0, 2, 3, 4)).reshape(KH, N_HEADS // 2, 2 * GRID_W, KH * GRID_W)
    return tab


def _attention(qk, v, bias, b, s):
    rows = s // GRID_W
    assert rows % ROWS_PER_ATT_BLOCK == 0 and rows >= ATT_WIN_ROWS
    qk4 = qk.reshape(b, rows, GRID_W, 2 * ATT_W)
    v4 = v.reshape(b, rows, GRID_W, ATT_W)
    rpb_blocks = rows // ROWS_PER_ATT_BLOCK

    def q_map(bi, rb, rr):
        return (bi, rb * ROWS_PER_ATT_BLOCK + rr, 0, 0)

    def k_map(bi, rb, rr):
        return (bi, _att_window_start(rb, rows), 0, ATT_W)

    def v_map(bi, rb, rr):
        return (bi, _att_window_start(rb, rows), 0, 0)

    win_block = tuple(pl.Element(n) for n in (1, ATT_WIN_ROWS, GRID_W, ATT_W))

    def bias_map(bi, rb, rr):
        r = rb * ROWS_PER_ATT_BLOCK + rr
        return (r - _att_row_start(r, rows), 0, 0, 0)

    def o_map(bi, rb, rr):
        return (bi, rb * ROWS_PER_ATT_BLOCK + rr, 0, 0)

    out = pl.pallas_call(
        functools.partial(_attn_kernel, rows=rows),
        grid=(b, rpb_blocks, ROWS_PER_ATT_BLOCK),
        in_specs=[
            pl.BlockSpec((None, None, GRID_W, ATT_W), q_map),
            pl.BlockSpec(win_block, k_map),
            pl.BlockSpec(win_block, v_map),
            pl.BlockSpec((None, N_HEADS // 2, 2 * GRID_W, KH * GRID_W), bias_map),
        ],
        out_specs=pl.BlockSpec((None, None, GRID_W, ATT_W), o_map),
        out_shape=jax.ShapeDtypeStruct((b, rows, GRID_W, ATT_W), bf16),
        compiler_params=_params("parallel", "parallel", "arbitrary"),
        name="natten",
    )(qk4, qk4, v4, bias)
    return out.reshape(b * s, ATT_W)


CONV_HALO = 16
CONV_CHUNK = 16


def _conv_kernel(prev_ref, main_ref, next_ref, w_ref, b_ref, g_ref, beta_ref, o_ref, buf_ref, *, ts, nblk):
    i = pl.program_id(1)
    buf_ref[0:CONV_HALO, :] = jnp.where(i > 0, prev_ref[...].astype(f32), 0.0)
    buf_ref[CONV_HALO:CONV_HALO + ts, :] = main_ref[...].astype(f32)
    buf_ref[CONV_HALO + ts:2 * CONV_HALO + ts, :] = jnp.where(i < nblk - 1, next_ref[...].astype(f32), 0.0)
    first = CONV_HALO - CONV_K // 2
    for c in range(ts // CONV_CHUNK):
        t0 = c * CONV_CHUNK
        acc = buf_ref[t0 + first:t0 + first + CONV_CHUNK, :] * w_ref[0:1, :]
        for k in range(1, CONV_K):
            acc = acc + buf_ref[t0 + first + k:t0 + first + k + CONV_CHUNK, :] * w_ref[k:k + 1, :]
        y = acc + b_ref[...]
        mu = jnp.mean(y, axis=-1, keepdims=True)
        yc = y - mu
        var = jnp.mean(yc * yc, axis=-1, keepdims=True)
        yn = yc * lax.rsqrt(var + EPS) * g_ref[...] + beta_ref[...]
        o_ref[t0:t0 + CONV_CHUNK, :] = (yn * _sigmoid(yn)).astype(o_ref.dtype)


def _conformer_conv(u, w_dw, b_dw, ln_g, ln_b, b, s, ts):
    c = u.shape[-1]
    u3 = u.reshape(b, s, c)
    nblk = s // ts
    halo_per_blk = ts // CONV_HALO
    n_halo = s // CONV_HALO
    row = lambda a: a.reshape(1, c).astype(f32)
    out = pl.pallas_call(
        functools.partial(_conv_kernel, ts=ts, nblk=nblk),
        grid=(b, nblk),
        in_specs=[
            pl.BlockSpec((None, CONV_HALO, c), lambda bi, i: (bi, jnp.maximum(i * halo_per_blk - 1, 0), 0)),
            pl.BlockSpec((None, ts, c), lambda bi, i: (bi, i, 0)),
            pl.BlockSpec((None, CONV_HALO, c), lambda bi, i: (bi, jnp.minimum((i + 1) * halo_per_blk, n_halo - 1), 0)),
            pl.BlockSpec((CONV_K, c), lambda bi, i: (0, 0)),
            pl.BlockSpec((1, c), lambda bi, i: (0, 0)),
            pl.BlockSpec((1, c), lambda bi, i: (0, 0)),
            pl.BlockSpec((1, c), lambda bi, i: (0, 0)),
        ],
        out_specs=pl.BlockSpec((None, ts, c), lambda bi, i: (bi, i, 0)),
        out_shape=jax.ShapeDtypeStruct((b, s, c), bf16),
        scratch_shapes=[pltpu.VMEM((ts + 2 * CONV_HALO, c), f32)],
        compiler_params=_params("parallel", "parallel"),
        name="conformer_conv",
    )(u3, u3, u3, w_dw.astype(f32), row(b_dw), row(ln_g), row(ln_b))
    return out.reshape(b * s, c)


LANES = 128


def _router_kernel(x_ref, g_ref, wrt_ref, wr_ref, h_ref, aff_ref, *, n_exp):
    d = x_ref.shape[-1]
    x = x_ref[...]
    ms = jnp.mean(x * x, axis=-1, keepdims=True)
    h = x * lax.rsqrt(ms + EPS) * g_ref[...]
    h_ref[:, 0:d] = h
    logits = lax.dot_general(wrt_ref[...], h, (((1,), (1,)), ((), ())),
                             preferred_element_type=f32, precision=lax.Precision.HIGHEST)
    m = jnp.max(logits, axis=0, keepdims=True)
    e = jnp.exp(logits - m)
    aff_ref[...] = e / jnp.sum(e, axis=0, keepdims=True)
    lt = jnp.dot(h, wr_ref[...], preferred_element_type=f32, precision=lax.Precision.HIGHEST)
    lane = lax.broadcasted_iota(jnp.int32, lt.shape, 1)
    lt = jnp.where(lane < n_exp, lt, NEG_INF)
    et = jnp.exp(lt - jnp.max(lt, axis=1, keepdims=True))
    h_ref[:, d:d + LANES] = et / jnp.sum(et, axis=1, keepdims=True)


def _router(x, g, w_router, b, s, tm):
    t, d = x.shape
    e = w_router.shape[-1]
    blocks_per_seq = s // tm
    wr_pad = jnp.pad(w_router, ((0, 0), (0, LANES - e)))
    h, aff = pl.pallas_call(
        functools.partial(_router_kernel, n_exp=e),
        grid=(t // tm,),
        in_specs=[pl.BlockSpec((tm, d), lambda i: (i, 0)),
                  pl.BlockSpec((1, d), lambda i: (0, 0)),
                  pl.BlockSpec((e, d), lambda i: (0, 0)),
                  pl.BlockSpec((d, LANES), lambda i: (0, 0))],
        out_specs=[pl.BlockSpec((tm, d + LANES), lambda i: (i, 0)),
                   pl.BlockSpec((None, e, tm), lambda i: (i // blocks_per_seq, 0, i % blocks_per_seq))],
        out_shape=[jax.ShapeDtypeStruct((t, d + LANES), f32),
                   jax.ShapeDtypeStruct((b, e, s), f32)],
        compiler_params=_params("parallel"),
        name="router",
    )(x, g.reshape(1, d), jnp.transpose(w_router), wr_pad)
    return h, aff


def _total(x):
    return jnp.sum(jnp.sum(x, axis=1, keepdims=True), axis=0, keepdims=True)


def _topk_kernel(aff_ref, idx_ref, *, cap):
    v = aff_ref[0]
    nch = v.shape[0]
    bits = pltpu.bitcast(v, jnp.int32)

    def step(i, thr):
        cand = thr | jnp.left_shift(jnp.int32(1), 30 - i)
        cnt = _total(jnp.where(bits >= cand, 1, 0))
        return jnp.where(cnt >= cap, cand, thr)

    thr = lax.fori_loop(0, 31, step, jnp.zeros((1, 1), jnp.int32))
    gt = bits > thr
    eq = bits == thr
    need = (cap - _total(jnp.where(gt, 1, 0))).astype(f32)

    r128 = lax.broadcasted_iota(jnp.int32, (LANES, LANES), 0)
    c128 = lax.broadcasted_iota(jnp.int32, (LANES, LANES), 1)
    incl_lanes = jnp.where(r128 <= c128, 1.0, 0.0).astype(bf16)
    rch = lax.broadcasted_iota(jnp.int32, (nch, nch), 0)
    cch = lax.broadcasted_iota(jnp.int32, (nch, nch), 1)
    before = jnp.where(cch < rch, 1.0, 0.0).astype(bf16)
    incl_chunks = jnp.where(rch <= cch, 1.0, 0.0).astype(bf16)

    eq_b = jnp.where(eq, 1.0, 0.0).astype(bf16)
    incl_eq = jnp.dot(eq_b, incl_lanes, preferred_element_type=f32)
    tot_eq = jnp.broadcast_to(incl_eq[:, LANES - 1:LANES], (nch, LANES)).astype(bf16)
    rank_eq = jnp.dot(before, tot_eq, preferred_element_type=f32) + incl_eq - 1.0
    sel_b = jnp.where(gt, 1.0, jnp.where(eq, jnp.where(rank_eq < need, 1.0, 0.0), 0.0)).astype(bf16)

    incl_sel = jnp.dot(sel_b, incl_lanes, preferred_element_type=f32)
    tot_row = lax.dot_general(jnp.ones((8, LANES), bf16), sel_b, (((1,), (1,)), ((), ())),
                              preferred_element_type=f32)
    cum_row = jnp.dot(tot_row.astype(bf16), incl_chunks, preferred_element_type=f32)
    cum1 = cum_row[0:1, :]
    excl1 = cum1 - tot_row[0:1, :]

    c_col = lax.broadcasted_iota(jnp.int32, (cap, 1), 0).astype(f32)
    chunk_of_c = jnp.sum(jnp.where(cum1 <= c_col, 1.0, 0.0), axis=1, keepdims=True)
    lane_ch = lax.broadcasted_iota(jnp.int32, (cap, nch), 1).astype(f32)
    in_chunk = lane_ch == chunk_of_c
    off_c = jnp.sum(jnp.where(in_chunk, excl1, 0.0), axis=1, keepdims=True)
    prow = jnp.dot(jnp.where(in_chunk, 1.0, 0.0).astype(bf16), incl_sel.astype(bf16),
                   preferred_element_type=f32)
    lane_of_c = jnp.sum(jnp.where(prow <= c_col - off_c, 1.0, 0.0), axis=1, keepdims=True)
    idx_ref[0] = (chunk_of_c * LANES + lane_of_c).astype(jnp.int32)


def _topk_indices(aff, cap):
    b, e, s = aff.shape
    nch = s // LANES
    idx = pl.pallas_call(
        functools.partial(_topk_kernel, cap=cap),
        grid=(b * e,),
        in_specs=[pl.BlockSpec((1, nch, LANES), lambda i: (i, 0, 0))],
        out_specs=pl.BlockSpec((1, cap, 1), lambda i: (i, 0, 0)),
        out_shape=jax.ShapeDtypeStruct((b * e, cap, 1), jnp.int32),
        compiler_params=_params("parallel"),
        name="topk_select",
    )(aff.reshape(b * e, nch, LANES))
    return idx.reshape(b, e, cap)


MOE_ROWS = 1024
MOE_RMW_ROWS = 256


def _moe_kernel(idx_ref, h_hbm, w1_ref, w3_ref, w2_ref, x_in_hbm, x_hbm,
                xg_ref, xb_ref, acc_ref, gate_ref, sem_g, sem_r, sem_w, *, seq, tc, rmw, n_f):
    del x_in_hbm
    d = x_hbm.shape[-1]
    fi = pl.program_id(3)
    base = pl.program_id(1) * seq

    def token(c):
        return base + idx_ref[0, 0, c]

    def loop(n, fn):
        def body(c, carry):
            fn(c)
            return carry
        lax.fori_loop(0, n, body, 0)

    def gather_copy(c):
        return pltpu.make_async_copy(h_hbm.at[pl.ds(token(c), 1)], xg_ref.at[pl.ds(c, 1)], sem_g)

    @pl.when(fi == 0)
    def _gather():
        loop(tc, lambda c: gather_copy(c).start())
        loop(tc, lambda c: gather_copy(c).wait())
        xb_ref[...] = xg_ref[:, 0:d].astype(bf16)
        affs = xg_ref[:, d:d + LANES]
        lane = lax.broadcasted_iota(jnp.int32, affs.shape, 1)
        gate_ref[...] = jnp.sum(jnp.where(lane == pl.program_id(0), affs, 0.0), axis=1, keepdims=True)

    xb = xb_ref[...]
    h1 = jnp.dot(xb, w1_ref[...], preferred_element_type=f32)
    h3 = jnp.dot(xb, w3_ref[...], preferred_element_type=f32)
    hid = (h1 * _sigmoid(h1) * h3).astype(bf16)
    contrib = jnp.dot(hid, w2_ref[...], preferred_element_type=f32)

    @pl.when(fi == 0)
    def _init():
        acc_ref[...] = contrib

    @pl.when(fi > 0)
    def _accumulate():
        acc_ref[...] += contrib

    @pl.when(fi == n_f - 1)
    def _scatter_add():
        def read_copy(c0, c):
            return pltpu.make_async_copy(x_hbm.at[pl.ds(token(c0 + c), 1)],
                                         xg_ref.at[pl.ds(c0 + c, 1), pl.ds(0, d)], sem_r)

        def write_copy(c0, c):
            return pltpu.make_async_copy(xg_ref.at[pl.ds(c0 + c, 1), pl.ds(0, d)],
                                         x_hbm.at[pl.ds(token(c0 + c), 1)], sem_w)

        for chunk in range(tc // rmw):
            c0 = chunk * rmw
            rows = pl.ds(c0, rmw)
            loop(rmw, lambda c: read_copy(c0, c).start())
            loop(rmw, lambda c: read_copy(c0, c).wait())
            xg_ref[rows, 0:d] = xg_ref[rows, 0:d] + acc_ref[rows, :] * gate_ref[rows, :]
            loop(rmw, lambda c: write_copy(c0, c).start())
            loop(rmw, lambda c: write_copy(c0, c).wait())


def _moe(x, h, idx, w1, w3, w2, b, s, tf):
    t, d = x.shape
    e, _, f = w1.shape
    cap = idx.shape[-1]
    n_f = f // tf
    tc = _tile(cap, MOE_ROWS)
    rmw = _tile(tc, MOE_RMW_ROWS)
    n_c = cap // tc
    kern = functools.partial(_moe_kernel, seq=s, tc=tc, rmw=rmw, n_f=n_f)
    return pl.pallas_call(
        kern,
        grid=(e, b, n_c, n_f),
        in_specs=[
            pl.BlockSpec((1, 1, tc), lambda ei, bi, ci, fi: ((bi * e + ei) * n_c + ci, 0, 0),
                         memory_space=pltpu.SMEM),
            pl.BlockSpec(memory_space=pl.ANY),
            pl.BlockSpec((None, d, tf), lambda ei, bi, ci, fi: (ei, 0, fi)),
            pl.BlockSpec((None, d, tf), lambda ei, bi, ci, fi: (ei, 0, fi)),
            pl.BlockSpec((None, tf, d), lambda ei, bi, ci, fi: (ei, fi, 0)),
            pl.BlockSpec(memory_space=pl.ANY),
        ],
        out_specs=pl.BlockSpec(memory_space=pl.ANY),
        out_shape=jax.ShapeDtypeStruct((t, d), f32),
        scratch_shapes=[
            pltpu.VMEM((tc, d + LANES), f32),
            pltpu.VMEM((tc, d), bf16),
            pltpu.VMEM((tc, d), f32),
            pltpu.VMEM((tc, 1), f32),
            pltpu.SemaphoreType.DMA(()),
            pltpu.SemaphoreType.DMA(()),
            pltpu.SemaphoreType.DMA(()),
        ],
        input_output_aliases={5: 0},
        compiler_params=_params("arbitrary", "arbitrary", "arbitrary", "arbitrary"),
        name="moe_ffn",
    )(idx.reshape(b * e * n_c, 1, tc), h, w1, w3, w2, x)


def _ple_kernel(x_ref, p_ref, wpg_ref, wpe_ref, g_ref, o_ref):
    x = x_ref[...]
    ms = jnp.mean(x * x, axis=-1, keepdims=True)
    xn = (x * lax.rsqrt(ms + EPS)).astype(bf16)
    gate = _sigmoid(jnp.dot(xn, wpg_ref[...], preferred_element_type=f32))
    pe = jnp.dot(p_ref[...].astype(bf16), wpe_ref[...], preferred_element_type=f32)
    pms = jnp.mean(pe * pe, axis=-1, keepdims=True)
    o_ref[...] = x + gate * (pe * lax.rsqrt(pms + EPS) * g_ref[...])


def _ple(x, p, w_pg, w_pe, g, tm):
    t, d = x.shape
    kp = p.shape[-1]
    return pl.pallas_call(
        _ple_kernel,
        grid=(t // tm,),
        in_specs=[pl.BlockSpec((tm, d), lambda i: (i, 0)),
                  pl.BlockSpec((tm, kp), lambda i: (i, 0)),
                  pl.BlockSpec((d, d), lambda i: (0, 0)),
                  pl.BlockSpec((kp, d), lambda i: (0, 0)),
                  pl.BlockSpec((1, d), lambda i: (0, 0))],
        out_specs=pl.BlockSpec((tm, d), lambda i: (i, 0)),
        out_shape=jax.ShapeDtypeStruct((t, d), f32),
        compiler_params=_params("parallel"),
        name="ple",
    )(x, p, w_pg, w_pe, g.reshape(1, d))


def _tile(n, pref):
    return pref if n % pref == 0 else n


def kernel(x, p, norm1_g, w_in, q_norm_g, k_norm_g, rpb, w_dw, b_dw, conv_ln_g, conv_ln_b, w_att_o,
           w_conv_o, w_out, norm2_g, w_router, w1, w3, w2, w_pe, pe_norm_g, w_pg):
    b, s, d = x.shape
    depth = w_in.shape[0]
    t = b * s
    e = w_router.shape[-1]
    f = w1.shape[-1]
    cap = CAP_FACTOR * s // e
    o_v, o_a, o_b, o_g = 2 * ATT_W, 3 * ATT_W, 3 * ATT_W + CONV_W, 3 * ATT_W + 2 * CONV_W

    tm = _tile(t, 1024)
    tn = 512
    tn_d = _tile(d, tn)
    head_ones = jnp.asarray(np.kron(np.eye(tn // HEAD_DIM), np.ones((HEAD_DIM, HEAD_DIM))), bf16)
    scale = 1.0 / math.sqrt(HEAD_DIM)

    x = x.reshape(t, d)
    for i in range(depth):
        w_in_i = w_in[i].astype(bf16)
        h = _rmsnorm(x, norm1_g[i], _tile(t, 512))

        qk_gain = jnp.concatenate([jnp.tile(q_norm_g[i], N_HEADS) * scale,
                                   jnp.tile(k_norm_g[i], N_HEADS)]).reshape(1, 2 * ATT_W).astype(f32)
        qk = _matmul([(h, w_in_i, 0)], [(qk_gain, "row", 0), (head_ones, "whole", 0)],
                     _epi_qknorm, 2 * ATT_W, bf16, tm, tn, "proj_qk")
        v = _matmul([(h, w_in_i, o_v // tn)], [], _epi_plain, ATT_W, bf16, tm, tn, "proj_v")
        u = _matmul([(h, w_in_i, o_a // tn), (h, w_in_i, o_b // tn)], [], _epi_glu,
                    CONV_W, bf16, tm, tn, "proj_glu")
        gates = _matmul([(h, w_in_i, o_g // tn_d)], [], _epi_sigmoid, 2 * d, bf16, tm, tn_d, "proj_gates")

        att = _attention(qk, v, _attention_bias(rpb[i]), b, s)
        cv = _conformer_conv(u, w_dw[i], b_dw[i], conv_ln_g[i], conv_ln_b[i], b, s, _tile(s, 256))

        mix = _matmul([(att, w_att_o[i].astype(bf16), 0), (cv, w_conv_o[i].astype(bf16), 0)],
                      [(gates, "tile", 0), (gates, "tile", d // tn_d)],
                      _epi_mix, d, bf16, tm, tn_d, "mix")
        x = _matmul([(mix, w_out[i].astype(bf16), 0)], [(x, "tile", 0)], _epi_residual,
                    d, f32, tm, tn_d, "out_proj")

        h2, aff = _router(x, norm2_g[i], w_router[i], b, s, _tile(s, 512))
        idx = _topk_indices(aff, cap)
        x = _moe(x, h2, idx, w1[i].astype(bf16), w3[i].astype(bf16), w2[i].astype(bf16),
                 b, s, _tile(f, 512))

        x = _ple(x, p[i].reshape(t, -1), w_pg[i].astype(bf16), w_pe[i].astype(bf16), pe_norm_g[i],
                 _tile(t, 256))
    return x.reshape(b, s, d)
```
